```python
import jax, jax.numpy as jnp
from jax import lax
import numpy as np

D_MODEL = 1024
BATCH = 2
SEQ = 8192
DEPTH = 1

D_A = D_MODEL
D_B = D_MODEL
CONV_A_WIDTH = 3
CONV_B_WIDTH = 31
D_FF = 4 * D_MODEL
RMS_EPS = 1e-6
LN_EPS = 1e-5
N_IN = 3 * D_A + 2 * D_B + 2 * D_MODEL

kernel_name = "hybrid_shortconv_conformer_conv_block"


def rms_norm(x, g):
    xf = x.astype(jnp.float32)
    y = xf * lax.rsqrt(jnp.mean(xf * xf, axis=-1, keepdims=True) + RMS_EPS)
    return (y * g.astype(jnp.float32)).astype(x.dtype)


def layer_norm(x, g, b):
    xf = x.astype(jnp.float32)
    mu = jnp.mean(xf, axis=-1, keepdims=True)
    var = jnp.mean(jnp.square(xf - mu), axis=-1, keepdims=True)
    y = (xf - mu) * lax.rsqrt(var + LN_EPS)
    return (y * g.astype(jnp.float32) + b.astype(jnp.float32)).astype(x.dtype)


def depthwise_conv_centred(u, w, b):
    k, c = w.shape
    pad = (k - 1) // 2
    y = lax.conv_general_dilated(
        u, w[:, None, :].astype(u.dtype), window_strides=(1,), padding=[(pad, pad)],
        dimension_numbers=("NWC", "WIO", "NWC"), feature_group_count=c)
    return y + b.astype(u.dtype)


def setup_inputs(seed: int = 0) -> dict:
    key = jax.random.key(seed)
    ks = jax.random.split(key, 24)
    f32 = jnp.float32

    def nrm(k, shape, scale):
        return jax.random.normal(k, shape, f32) * scale

    def gain(k, n):
        return jnp.ones((n,), f32) + 0.05 * jax.random.normal(k, (n,), f32)

    return {
        "x": jax.random.normal(ks[0], (BATCH, SEQ, D_MODEL), f32),
        "norm1_pre_g": gain(ks[1], D_MODEL),
        "w_in": nrm(ks[2], (D_MODEL, N_IN), D_MODEL ** -0.5),
        "b_in": nrm(ks[3], (N_IN,), 0.02),
        "conv_a_w": nrm(ks[4], (CONV_A_WIDTH, D_A), CONV_A_WIDTH ** -0.5),
        "conv_a_b": nrm(ks[5], (D_A,), 0.02),
        "w_a_out": nrm(ks[6], (D_A, D_MODEL), D_A ** -0.5),
        "conv_b_w": nrm(ks[7], (CONV_B_WIDTH, D_B), CONV_B_WIDTH ** -0.5),
        "conv_b_b": nrm(ks[8], (D_B,), 0.02),
        "ln_b_g": gain(ks[9], D_B),
        "ln_b_b": nrm(ks[10], (D_B,), 0.02),
        "w_b_out": nrm(ks[11], (D_B, D_MODEL), D_B ** -0.5),
        "w_o": nrm(ks[12], (D_MODEL, D_MODEL), D_MODEL ** -0.5),
        "norm1_post_g": gain(ks[13], D_MODEL),
        "norm2_pre_g": gain(ks[14], D_MODEL),
        "w_mlp_in": nrm(ks[15], (D_MODEL, D_FF), D_MODEL ** -0.5),
        "w_mlp_out": nrm(ks[16], (D_FF, D_MODEL), D_FF ** -0.5),
        "norm2_post_g": gain(ks[17], D_MODEL),
    }


def reference(x, norm1_pre_g, w_in, b_in, conv_a_w, conv_a_b, w_a_out,
              conv_b_w, conv_b_b, ln_b_g, ln_b_b, w_b_out, w_o, norm1_post_g,
              norm2_pre_g, w_mlp_in, w_mlp_out, norm2_post_g):
    for _ in range(DEPTH):
        h = rms_norm(x, norm1_pre_g)
        proj = jnp.einsum("bsd,dn->bsn", h, w_in) + b_in
        splits = np.cumsum([D_A, D_A, D_A, D_B, D_B, D_MODEL])
        b_gate, c_gate, h_a, a_b, g_b, z_a, z_b = jnp.split(proj, splits, axis=-1)

        v_a = depthwise_conv_centred(c_gate * h_a, conv_a_w, conv_a_b)
        y_a = jnp.einsum("bsc,cd->bsd", b_gate * v_a, w_a_out)

        u_b = a_b * jax.nn.sigmoid(g_b)
        v_b = depthwise_conv_centred(u_b, conv_b_w, conv_b_b)
        v_b = jax.nn.silu(layer_norm(v_b, ln_b_g, ln_b_b))
        y_b = jnp.einsum("bsc,cd->bsd", v_b, w_b_out)

        merged = jax.nn.sigmoid(z_a) * y_a + jax.nn.sigmoid(z_b) * y_b
        mix_out = jnp.einsum("bsd,de->bse", merged, w_o)
        x = x + rms_norm(mix_out, norm1_post_g)

        h2 = rms_norm(x, norm2_pre_g)
        f = jnp.square(jax.nn.relu(jnp.einsum("bsd,df->bsf", h2, w_mlp_in)))
        f = jnp.einsum("bsf,fd->bsd", f, w_mlp_out)
        x = x + rms_norm(f, norm2_post_g)
    return x
```

```python
import functools

import jax
import jax.numpy as jnp
from jax import lax
from jax.experimental import pallas as pl
from jax.experimental.pallas import tpu as pltpu

RMS_EPS = 1e-6
LN_EPS = 1e-5

LANES = 128
BF16_SUBLANES = 16
HALO = BF16_SUBLANES
COL_CHUNK = 256
FF_CHUNK = 1024
VMEM_LIMIT_BYTES = 56 * 1024 * 1024


def _rms(x, g):
    ms = jnp.mean(x * x, axis=-1, keepdims=True)
    return x * lax.rsqrt(ms + RMS_EPS) * g


def _dot(a, b):
    return jnp.dot(a, b, preferred_element_type=jnp.float32)


def _mixer_kernel(xp_ref, x_ref, xn_ref, g1_ref, win_ref, bin_ref, caw_ref, cab_ref,
                  waout_ref, cbw_ref, cbb_ref, lng_ref, lnb_ref, wbout_ref, wo_ref,
                  g1p_ref, o_ref, h_scr, ua_scr, ub_scr, vb_scr, ga_scr,
                  *, ts, seq, d, ka, kb):
    ext = ts + 2 * HALO
    i = pl.program_id(0)
    g1 = g1_ref[...]

    h_scr[0:HALO, :] = _rms(xp_ref[...], g1).astype(jnp.bfloat16)
    h_scr[HALO:HALO + ts, :] = _rms(x_ref[...], g1).astype(jnp.bfloat16)
    h_scr[HALO + ts:ext, :] = _rms(xn_ref[...], g1).astype(jnp.bfloat16)

    pos = (i * ts) % seq - HALO + lax.broadcasted_iota(jnp.int32, (ext, 1), 0)
    valid = (pos >= 0) & (pos < seq)

    off_bg, off_cg, off_ha, off_a, off_g, off_za, off_zb = (n * d for n in range(7))

    def proj(rows, off, width):
        return _dot(rows, win_ref[:, off:off + width]) + bin_ref[:, off:off + width]

    def conv(u_scr, blk, w_ref, b_ref, cs, k_taps):
        first = HALO - (k_taps - 1) // 2
        acc = jnp.broadcast_to(b_ref[:, cs:cs + LANES], (ts, LANES))
        for k in range(k_taps):
            acc = acc + u_scr[blk, pl.ds(first + k, ts), :] * w_ref[k:k + 1, cs:cs + LANES]
        return acc

    n_blk = COL_CHUNK // LANES
    for j in range(d // COL_CHUNK):
        c0 = j * COL_CHUNK
        a = proj(h_scr[...], off_a + c0, COL_CHUNK)
        g = proj(h_scr[...], off_g + c0, COL_CHUNK)
        ub = jnp.where(valid, a * jax.nn.sigmoid(g), 0.0)
        for b in range(n_blk):
            ub_scr[j * n_blk + b] = ub[:, b * LANES:(b + 1) * LANES]
        for b in range(n_blk):
            cs = c0 + b * LANES
            vb_scr[:, cs:cs + LANES] = conv(ub_scr, j * n_blk + b, cbw_ref, cbb_ref, cs, kb)

        cg = proj(h_scr[...], off_cg + c0, COL_CHUNK)
        ha = proj(h_scr[...], off_ha + c0, COL_CHUNK)
        ua = jnp.where(valid, cg * ha, 0.0)
        for b in range(n_blk):
            ua_scr[j * n_blk + b] = ua[:, b * LANES:(b + 1) * LANES]
        bg = proj(h_scr[HALO:HALO + ts, :], off_bg + c0, COL_CHUNK)
        for b in range(n_blk):
            cs = c0 + b * LANES
            va = conv(ua_scr, j * n_blk + b, caw_ref, cab_ref, cs, ka)
            ga_scr[:, cs:cs + LANES] = (bg[:, b * LANES:(b + 1) * LANES] * va).astype(jnp.bfloat16)

    vb = vb_scr[...]
    mu = jnp.mean(vb, axis=-1, keepdims=True)
    dv = vb - mu
    var = jnp.mean(dv * dv, axis=-1, keepdims=True)
    ln = dv * lax.rsqrt(var + LN_EPS) * lng_ref[...] + lnb_ref[...]
    act = ln * jax.nn.sigmoid(ln)
    y_b = _dot(act.astype(jnp.bfloat16), wbout_ref[...])
    y_a = _dot(ga_scr[...], waout_ref[...])

    h_main = h_scr[HALO:HALO + ts, :]
    z_a = proj(h_main, off_za, d)
    z_b = proj(h_main, off_zb, d)
    merged = jax.nn.sigmoid(z_a) * y_a + jax.nn.sigmoid(z_b) * y_b
    mix = _dot(merged.astype(jnp.bfloat16), wo_ref[...])
    o_ref[...] = x_ref[...] + _rms(mix, g1p_ref[...])


def _mlp_kernel(x_ref, g2_ref, w1_ref, w2_ref, g2p_ref, o_ref, *, dff):
    x1 = x_ref[...]
    h2 = _rms(x1, g2_ref[...]).astype(jnp.bfloat16)
    acc = jnp.zeros(x1.shape, jnp.float32)
    for j in range(dff // FF_CHUNK):
        f0 = j * FF_CHUNK
        f = jnp.maximum(_dot(h2, w1_ref[:, f0:f0 + FF_CHUNK]), 0.0)
        acc = acc + _dot((f * f).astype(jnp.bfloat16), w2_ref[f0:f0 + FF_CHUNK, :])
    o_ref[...] = x1 + _rms(acc, g2p_ref[...])


def _resident(shape):
    return pl.BlockSpec(shape, lambda i: (0,) * len(shape), pipeline_mode=pl.Buffered(1))


def _row(v):
    return v.reshape(1, -1)


def _token_tile(seq):
    ts = 256
    assert seq % ts == 0 and ts % HALO == 0
    return ts


def kernel(x, norm1_pre_g, w_in, b_in, conv_a_w, conv_a_b, w_a_out, conv_b_w, conv_b_b, ln_b_g, ln_b_b, w_b_out, w_o, norm1_post_g, norm2_pre_g, w_mlp_in, w_mlp_out, norm2_post_g):
    batch, seq, d = x.shape
    n_tok = batch * seq
    ka, kb = conv_a_w.shape[0], conv_b_w.shape[0]
    dff = w_mlp_in.shape[1]
    assert w_in.shape == (d, 7 * d) and d % COL_CHUNK == 0 and dff % FF_CHUNK == 0
    assert ka % 2 == 1 and kb % 2 == 1 and (kb - 1) // 2 < HALO
    ts = _token_tile(seq)
    ext = ts + 2 * HALO
    n_tiles = n_tok // ts
    halo_per_tile = ts // HALO
    n_halo_blocks = n_tok // HALO

    x2 = x.reshape(n_tok, d)
    bf = jnp.bfloat16
    params = pltpu.CompilerParams(dimension_semantics=("parallel",), vmem_limit_bytes=VMEM_LIMIT_BYTES)
    tile = pl.BlockSpec((ts, d), lambda i: (i, 0))

    x1 = pl.pallas_call(
        functools.partial(_mixer_kernel, ts=ts, seq=seq, d=d, ka=ka, kb=kb),
        grid=(n_tiles,),
        in_specs=[
            pl.BlockSpec((HALO, d), lambda i: (jnp.maximum(i * halo_per_tile - 1, 0), 0)),
            tile,
            pl.BlockSpec((HALO, d), lambda i: (jnp.minimum((i + 1) * halo_per_tile, n_halo_blocks - 1), 0)),
            _resident((1, d)), _resident((d, 7 * d)), _resident((1, 7 * d)),
            _resident((ka, d)), _resident((1, d)), _resident((d, d)),
            _resident((kb, d)), _resident((1, d)), _resident((1, d)), _resident((1, d)),
            _resident((d, d)), _resident((d, d)), _resident((1, d)),
        ],
        out_specs=tile,
        out_shape=jax.ShapeDtypeStruct((n_tok, d), jnp.float32),
        scratch_shapes=[
            pltpu.VMEM((ext, d), bf),
            pltpu.VMEM((d // LANES, ext, LANES), jnp.float32),
            pltpu.VMEM((d // LANES, ext, LANES), jnp.float32),
            pltpu.VMEM((ts, d), jnp.float32),
            pltpu.VMEM((ts, d), bf),
        ],
        compiler_params=params,
        name="mixer",
    )(x2, x2, x2, _row(norm1_pre_g), w_in.astype(bf), _row(b_in), conv_a_w, _row(conv_a_b),
      w_a_out.astype(bf), conv_b_w, _row(conv_b_b), _row(ln_b_g), _row(ln_b_b),
      w_b_out.astype(bf), w_o.astype(bf), _row(norm1_post_g))

    y = pl.pallas_call(
        functools.partial(_mlp_kernel, dff=dff),
        grid=(n_tiles,),
        in_specs=[tile, _resident((1, d)), _resident((d, dff)), _resident((dff, d)), _resident((1, d))],
        out_specs=tile,
        out_shape=jax.ShapeDtypeStruct((n_tok, d), jnp.float32),
        compiler_params=params,
        name="mlp",
    )(x1, _row(norm2_pre_g), w_mlp_in.astype(bf), w_mlp_out.astype(bf), _row(norm2_post_g))
    return y.reshape(batch, seq, d)
```
